```python
import math
import jax, jax.numpy as jnp
from jax import lax
import numpy as np

D_MODEL = 1024
BATCH = 16
SEQ = 2048
DEPTH = 4

D_MIX = 2 * D_MODEL
HEAD_DIM = 64
A_WIDTH = D_MIX // 4
A_HEADS = A_WIDTH // HEAD_DIM
B_WIDTH = D_MIX // 4
B_HEADS = B_WIDTH // HEAD_DIM
C_WIDTH = D_MIX // 2
C_HEADS = C_WIDTH // HEAD_DIM
CONV_A_WIDTH = 31
GMLP_CHUNK = 128
SSM_STATE = 128
SSM_GROUPS = 2
SSM_CONV = 4
SSD_CHUNK = 128
D_CONV_C = C_WIDTH + 2 * SSM_GROUPS * SSM_STATE
D_IN_PROJ = 2 * A_WIDTH + 2 * B_WIDTH + C_WIDTH + D_CONV_C + C_HEADS
D_FF = 4 * D_MODEL
EPS = 1e-5

kernel_name = "hybrid_conv_gmlp_ssd_trunk"


def rmsnorm(x, g):
    xf = x.astype(jnp.float32)
    y = xf * lax.rsqrt(jnp.mean(xf * xf, axis=-1, keepdims=True) + EPS)
    return (y * g.astype(jnp.float32)).astype(x.dtype)


def head_layernorm(x, g, b, n_heads):
    lead = x.shape[:-1]
    xf = x.astype(jnp.float32).reshape(*lead, n_heads, -1)
    mu = jnp.mean(xf, axis=-1, keepdims=True)
    xc = xf - mu
    var = jnp.mean(xc * xc, axis=-1, keepdims=True)
    y = (xc * lax.rsqrt(var + EPS)).reshape(*lead, -1)
    return (y * g.astype(jnp.float32) + b.astype(jnp.float32)).astype(x.dtype)


def causal_dwconv(x, w, b):
    k, c = w.shape
    y = lax.conv_general_dilated(
        x, w[:, None, :].astype(x.dtype), window_strides=(1,), padding=[(k - 1, 0)],
        dimension_numbers=("NWC", "WIO", "NWC"), feature_group_count=c)
    return y + b.astype(x.dtype)


def conformer_mixer(a_val, a_gate, conv_w, conv_b, ln_g, ln_b):
    h = a_val * jax.nn.sigmoid(a_gate)
    h = causal_dwconv(h, conv_w, conv_b)
    h = head_layernorm(h, ln_g, ln_b, A_HEADS)
    return jax.nn.silu(h)


def gmlp_mixer(u, v, ln_g, ln_b, w_s, b_s):
    u = jax.nn.gelu(u, approximate=False)
    v = jax.nn.gelu(v, approximate=False)
    v = head_layernorm(v, ln_g, ln_b, B_HEADS)
    bsz, s, _ = v.shape
    nc = s // GMLP_CHUNK
    v = v.reshape(bsz, nc, GMLP_CHUNK, B_HEADS, HEAD_DIM)
    mask = jnp.tril(jnp.ones((GMLP_CHUNK, GMLP_CHUNK), dtype=bool))
    w = jnp.where(mask, w_s, jnp.zeros_like(w_s))
    mix = jnp.einsum("hts,bcshp->bcthp", w, v) + b_s.T[:, :, None]
    return (u.reshape(v.shape) * mix).reshape(bsz, s, B_WIDTH)


def ssd_chunked(x, dt, A, B, C):
    bsz, s, h, p = x.shape
    g, n = B.shape[2], B.shape[3]
    hg = h // g
    nc = s // SSD_CHUNK
    l = SSD_CHUNK
    xc = (x * dt[..., None]).reshape(bsz, nc, l, g, hg, p)
    a_cs = jnp.cumsum((dt * A).reshape(bsz, nc, l, g, hg), axis=2)
    Bc = B.reshape(bsz, nc, l, g, n)
    Cc = C.reshape(bsz, nc, l, g, n)
    a_t = jnp.moveaxis(a_cs, 2, -1)
    seg = a_t[..., :, None] - a_t[..., None, :]
    mask = jnp.tril(jnp.ones((l, l), dtype=bool))
    L = jnp.exp(jnp.where(mask, seg, -jnp.inf))
    CB = jnp.einsum("bclgn,bcsgn->bcgls", Cc, Bc)
    y_diag = jnp.einsum("bcghls,bcsghp->bclghp", CB[:, :, :, None] * L, xc)
    decay_states = jnp.exp(a_cs[:, :, -1:] - a_cs)
    states = jnp.einsum("bclgn,bclghp->bcghpn", Bc, xc * decay_states[..., None])
    chunk_decay = jnp.exp(a_cs[:, :, -1])

    def step(carry, inp):
        st, dec = inp
        return carry * dec[..., None, None] + st, carry

    init = jnp.zeros((bsz, g, hg, p, n), x.dtype)
    _, prev = lax.scan(step, init, (jnp.moveaxis(states, 1, 0), jnp.moveaxis(chunk_decay, 1, 0)))
    prev = jnp.moveaxis(prev, 0, 1)
    y_off = jnp.einsum("bclgn,bcghpn->bclghp", Cc, prev) * jnp.exp(a_cs)[..., None]
    return (y_diag + y_off).reshape(bsz, s, h, p)


def mamba2_mixer(z, xbc, dt_raw, conv_w, conv_b, dt_bias, a_log, d_skip, norm_g):
    xbc = jax.nn.silu(causal_dwconv(xbc, conv_w, conv_b))
    xs, Bm, Cm = jnp.split(xbc, [C_WIDTH, C_WIDTH + SSM_GROUPS * SSM_STATE], axis=-1)
    bsz, s, _ = xs.shape
    xs = xs.astype(jnp.float32).reshape(bsz, s, C_HEADS, HEAD_DIM)
    Bm = Bm.astype(jnp.float32).reshape(bsz, s, SSM_GROUPS, SSM_STATE)
    Cm = Cm.astype(jnp.float32).reshape(bsz, s, SSM_GROUPS, SSM_STATE)
    dt = jax.nn.softplus(dt_raw.astype(jnp.float32) + dt_bias.astype(jnp.float32))
    A = -jnp.exp(a_log.astype(jnp.float32))
    y = ssd_chunked(xs, dt, A, Bm, Cm) + d_skip.astype(jnp.float32)[:, None] * xs
    y = y.reshape(bsz, s, C_WIDTH) * jax.nn.silu(z.astype(jnp.float32))
    yg = y.reshape(bsz, s, SSM_GROUPS, -1)
    yg = yg * lax.rsqrt(jnp.mean(yg * yg, axis=-1, keepdims=True) + EPS)
    y = yg.reshape(bsz, s, C_WIDTH) * norm_g.astype(jnp.float32)
    return y.astype(z.dtype)


def setup_inputs(seed: int = 0) -> dict:
    key = jax.random.key(seed)
    ks = jax.random.split(key, 24)
    f32 = jnp.float32

    def nrm(k, shape, scale):
        return jax.random.normal(k, shape, f32) * scale

    dt0 = jnp.exp(jax.random.uniform(ks[13], (DEPTH, C_HEADS), f32) * (math.log(0.1) - math.log(0.001)) + math.log(0.001))
    return {
        "x": nrm(ks[0], (BATCH, SEQ, D_MODEL), 1.0),
        "norm1_g": 1.0 + nrm(ks[1], (DEPTH, D_MODEL), 0.02),
        "w_in": nrm(ks[2], (DEPTH, D_MODEL, D_IN_PROJ), D_MODEL ** -0.5),
        "conv_a_w": nrm(ks[3], (DEPTH, CONV_A_WIDTH, A_WIDTH), CONV_A_WIDTH ** -0.5),
        "conv_a_b": nrm(ks[4], (DEPTH, A_WIDTH), 0.02),
        "ln_a_g": 1.0 + nrm(ks[5], (DEPTH, A_WIDTH), 0.02),
        "ln_a_b": nrm(ks[6], (DEPTH, A_WIDTH), 0.02),
        "ln_b_g": 1.0 + nrm(ks[7], (DEPTH, B_WIDTH), 0.02),
        "ln_b_b": nrm(ks[8], (DEPTH, B_WIDTH), 0.02),
        "w_spatial": nrm(ks[9], (DEPTH, B_HEADS, GMLP_CHUNK, GMLP_CHUNK), GMLP_CHUNK ** -0.5),
        "b_spatial": 1.0 + nrm(ks[10], (DEPTH, B_HEADS, GMLP_CHUNK), 0.1),
        "conv_c_w": nrm(ks[11], (DEPTH, SSM_CONV, D_CONV_C), SSM_CONV ** -0.5),
        "conv_c_b": nrm(ks[12], (DEPTH, D_CONV_C), 0.02),
        "dt_bias": dt0 + jnp.log(-jnp.expm1(-dt0)),
        "a_log": jnp.log(jax.random.uniform(ks[14], (DEPTH, C_HEADS), f32, 1.0, 16.0)),
        "d_skip": 1.0 + nrm(ks[15], (DEPTH, C_HEADS), 0.1),
        "norm_c_g": 1.0 + nrm(ks[16], (DEPTH, C_WIDTH), 0.02),
        "w_out": nrm(ks[17], (DEPTH, D_MIX, D_MODEL), D_MIX ** -0.5),
        "norm2_g": 1.0 + nrm(ks[18], (DEPTH, D_MODEL), 0.02),
        "w_ff1": nrm(ks[19], (DEPTH, D_MODEL, D_FF), D_MODEL ** -0.5),
        "w_ff2": nrm(ks[20], (DEPTH, D_FF, D_MODEL), D_FF ** -0.5),
        "final_g": 1.0 + nrm(ks[21], (D_MODEL,), 0.02),
    }


def reference(x, norm1_g, w_in, conv_a_w, conv_a_b, ln_a_g, ln_a_b, ln_b_g, ln_b_b,
              w_spatial, b_spatial, conv_c_w, conv_c_b, dt_bias, a_log, d_skip, norm_c_g,
              w_out, norm2_g, w_ff1, w_ff2, final_g):
    split_idx = [A_WIDTH, 2 * A_WIDTH, 2 * A_WIDTH + B_WIDTH, 2 * A_WIDTH + 2 * B_WIDTH,
                 2 * A_WIDTH + 2 * B_WIDTH + C_WIDTH, 2 * A_WIDTH + 2 * B_WIDTH + C_WIDTH + D_CONV_C]
    for i in range(DEPTH):
        h = rmsnorm(x, norm1_g[i])
        proj = h @ w_in[i]
        a_val, a_gate, b_u, b_v, z, xbc, dt_raw = jnp.split(proj, split_idx, axis=-1)
        ya = conformer_mixer(a_val, a_gate, conv_a_w[i], conv_a_b[i], ln_a_g[i], ln_a_b[i])
        yb = gmlp_mixer(b_u, b_v, ln_b_g[i], ln_b_b[i], w_spatial[i], b_spatial[i])
        yc = mamba2_mixer(z, xbc, dt_raw, conv_c_w[i], conv_c_b[i], dt_bias[i], a_log[i],
                          d_skip[i], norm_c_g[i])
        x = x + jnp.concatenate([ya, yb, yc], axis=-1) @ w_out[i]
        h = rmsnorm(x, norm2_g[i])
        x = x + jnp.square(jax.nn.relu(h @ w_ff1[i])) @ w_ff2[i]
    return rmsnorm(x, final_g)
```

```python
import functools

import jax
import jax.numpy as jnp
from jax import lax
from jax.experimental import pallas as pl
from jax.experimental.pallas import tpu as pltpu

HEAD_DIM = 64
SSM_GROUPS = 2
CHUNK = 128
EPS = 1e-5
LANES = 128
MXU_DIM = 256
VMEM_LIMIT_BYTES = 56 * 1024 * 1024

MIX_ROWS = 256
FFN_ROWS = 512
A_HIST = 32
C_HIST = 8

F32 = jnp.float32
BF16 = jnp.bfloat16


def _dot(a, b):
    return jnp.dot(a, b, preferred_element_type=F32)


def _silu(x):
    return x * jax.nn.sigmoid(x)


def _gelu(x):
    return 0.5 * x * (1.0 + lax.erf(x * 0.7071067811865476))


def _rmsnorm(x, g):
    return x * lax.rsqrt(jnp.mean(x * x, axis=-1, keepdims=True) + EPS) * g


def _head_layernorm(x, g, b, avg):
    outs = []
    for c0 in range(0, x.shape[1], MXU_DIM):
        xs = x[:, c0:c0 + MXU_DIM]
        mu = _dot(xs.astype(BF16), avg)
        xc = xs - mu
        var = _dot((xc * xc).astype(BF16), avg)
        outs.append(xc * lax.rsqrt(var + EPS))
    return jnp.concatenate(outs, axis=1) * g + b


def _split3(x):
    p1 = x.astype(BF16)
    r1 = x - p1.astype(F32)
    p2 = r1.astype(BF16)
    r2 = r1 - p2.astype(F32)
    return p1, p2, r2.astype(BF16)


def _mix_kernel(x_ref, g1_ref, win_ref, caw_ref, cab_ref, lag_ref, lab_ref, lbg_ref, lbb_ref,
                ws_ref, bmat_ref, ccw_ref, ccb_ref, dtb_ref, alog_ref, dsk_ref, ncg_ref, wout_ref,
                avg_ref, o_ref, proj, abuf, cbuf, state, ybuf, *, dims):
    T, aw, bw, cw, n_state, n_taps_a, n_taps_c = dims
    n_chunks = T // CHUNK
    gw = cw // SSM_GROUPS
    o_bu, o_bv, o_z = 2 * aw, 2 * aw + bw, 2 * aw + 2 * bw
    o_xbc = o_z + cw
    o_dt = o_xbc + cw + 2 * SSM_GROUPS * n_state

    @pl.when(pl.program_id(1) == 0)
    def _():
        abuf[0:A_HIST, :] = jnp.zeros((A_HIST, aw), F32)
        cbuf[0:C_HIST, :] = jnp.zeros((C_HIST, cbuf.shape[1]), F32)
        state[...] = jnp.zeros(state.shape, F32)

    x = x_ref[...]
    hb = _rmsnorm(x, g1_ref[...]).astype(BF16)
    proj[...] = _dot(hb, win_ref[...])
    avg = avg_ref[...]

    abuf[A_HIST:A_HIST + T, :] = proj[:, 0:aw] * jax.nn.sigmoid(proj[:, aw:2 * aw])
    acc = jnp.broadcast_to(cab_ref[...], (T, aw))
    for k in range(n_taps_a):
        acc = acc + caw_ref[k:k + 1, :] * abuf[pl.ds(A_HIST - (n_taps_a - 1) + k, T), :]
    abuf[0:A_HIST, :] = abuf[T:T + A_HIST, :]
    ya = _silu(_head_layernorm(acc, lag_ref[...], lab_ref[...], avg))
    ybuf[:, 0:aw] = ya.astype(BF16)

    lane = lax.broadcasted_iota(jnp.int32, (CHUNK, LANES), 1)
    row = lax.broadcasted_iota(jnp.int32, (CHUNK, LANES), 0)
    lo = lane < HEAD_DIM
    causal = row >= lane
    u = _gelu(proj[:, o_bu:o_bu + bw])
    vn = _head_layernorm(_gelu(proj[:, o_bv:o_bv + bw]), lbg_ref[...], lbb_ref[...], avg)
    for j in range(bw // LANES):
        w_pair = jnp.concatenate(
            [jnp.where(causal, ws_ref[2 * j], 0.0), jnp.where(causal, ws_ref[2 * j + 1], 0.0)],
            axis=1).astype(BF16)
        cols = slice(j * LANES, (j + 1) * LANES)
        for c in range(n_chunks):
            rows = slice(c * CHUNK, (c + 1) * CHUNK)
            vp = vn[rows, cols]
            rhs = jnp.concatenate([jnp.where(lo, vp, 0.0).astype(BF16),
                                   jnp.where(lo, 0.0, vp).astype(BF16)], axis=0)
            mixv = _dot(w_pair, rhs) + bmat_ref[:, cols]
            ybuf[rows, aw + j * LANES:aw + (j + 1) * LANES] = (u[rows, cols] * mixv).astype(BF16)

    cbuf[C_HIST:C_HIST + T, :] = proj[:, o_xbc:o_dt]
    xbc = jnp.broadcast_to(ccb_ref[...], (T, cbuf.shape[1]))
    for k in range(n_taps_c):
        xbc = xbc + ccw_ref[k:k + 1, :] * cbuf[pl.ds(C_HIST - (n_taps_c - 1) + k, T), :]
    cbuf[0:C_HIST, :] = cbuf[T:T + C_HIST, :]
    xbc = _silu(xbc)
    dt = jax.nn.softplus(proj[:, o_dt:o_dt + LANES] + dtb_ref[...])
    dta = dt * (-jnp.exp(alog_ref[...]))
    tri = jnp.where(causal, 1.0, 0.0).astype(BF16)
    n_pairs = cw // LANES
    pairs_per_group = n_pairs // SSM_GROUPS
    heads_valid = 2 * n_pairs
    for c in range(n_chunks):
        rows = slice(c * CHUNK, (c + 1) * CHUNK)
        p1, p2, p3 = _split3(dta[rows, :])
        a_cs = _dot(tri, p1) + _dot(tri, p2) + _dot(tri, p3)
        a_cs_t = a_cs.T[0:heads_valid, :]
        dt_t = dt[rows, :].T[0:heads_valid, :]
        last = jnp.broadcast_to(a_cs_t[:, CHUNK - 1:CHUNK], a_cs_t.shape)
        w_t = jnp.exp(last - a_cs_t) * dt_t
        for g in range(SSM_GROUPS):
            bm = xbc[rows, cw + g * n_state:cw + (g + 1) * n_state]
            cm = xbc[rows, cw + (SSM_GROUPS + g) * n_state:cw + (SSM_GROUPS + g + 1) * n_state]
            cb = lax.dot_general(cm.astype(BF16), bm.astype(BF16), (((1,), (1,)), ((), ())),
                                 preferred_element_type=F32)
            bm_t = bm.T
            ys = []
            ss = jnp.zeros((CHUNK, 1), F32)
            for jj in range(pairs_per_group):
                j = g * pairs_per_group + jj
                cols = slice(j * LANES, (j + 1) * LANES)
                m1, m2, m3t, cd = [], [], [], []
                for h in (2 * j, 2 * j + 1):
                    col_a = jnp.broadcast_to(a_cs[:, h:h + 1], (CHUNK, LANES))
                    seg = col_a - a_cs_t[h:h + 1, :]
                    decay = jnp.exp(jnp.where(causal, seg, -jnp.inf))
                    m1.append((cb * decay * dt_t[h:h + 1, :]).astype(BF16))
                    m2.append((cm * jnp.exp(col_a)).astype(BF16))
                    m3t.append((bm_t * w_t[h:h + 1, :]).astype(BF16))
                    cd.append(jnp.exp(col_a[CHUNK - 1:CHUNK, :]))
                xs = xbc[rows, cols]
                s_prev = state[:, cols]
                x_lo = jnp.where(lo, xs, 0.0).astype(BF16)
                x_hi = jnp.where(lo, 0.0, xs).astype(BF16)
                s_lo = jnp.where(lo, s_prev, 0.0).astype(BF16)
                s_hi = jnp.where(lo, 0.0, s_prev).astype(BF16)
                y = _dot(jnp.concatenate(m1 + m2, axis=1),
                         jnp.concatenate([x_lo, x_hi, s_lo, s_hi], axis=0))
                y = y + dsk_ref[:, cols] * xs
                y = y * _silu(proj[rows, o_z + j * LANES:o_z + (j + 1) * LANES])
                s_add = _dot(jnp.concatenate(m3t, axis=1), jnp.concatenate([x_lo, x_hi], axis=0))
                state[:, cols] = s_prev * jnp.where(lo[0:1, :], cd[0], cd[1]) + s_add
                ys.append(y)
                ss = ss + jnp.sum(y * y, axis=-1, keepdims=True)
            rstd = lax.rsqrt(ss * (1.0 / gw) + EPS)
            for jj in range(pairs_per_group):
                j = g * pairs_per_group + jj
                cols = slice(j * LANES, (j + 1) * LANES)
                ybuf[rows, aw + bw + j * LANES:aw + bw + (j + 1) * LANES] = (
                    ys[jj] * rstd * ncg_ref[:, cols]).astype(BF16)

    o_ref[...] = x + _dot(ybuf[...], wout_ref[...])


def _ffn_kernel(x_ref, g2_ref, w1_ref, w2_ref, gf_ref, o_ref, hbuf, *, ff_chunk, final_norm):
    x = x_ref[...]
    hb = _rmsnorm(x, g2_ref[...]).astype(BF16)
    for f0 in range(0, w1_ref.shape[1], ff_chunk):
        a = jnp.maximum(_dot(hb, w1_ref[:, f0:f0 + ff_chunk]), 0.0)
        hbuf[:, f0:f0 + ff_chunk] = (a * a).astype(BF16)
    y = x + _dot(hbuf[...], w2_ref[...])
    if final_norm:
        y = _rmsnorm(y, gf_ref[...])
    o_ref[...] = y


def _resident(shape, layer):
    nd = len(shape)
    return pl.BlockSpec((None,) + tuple(shape[1:]), lambda *_: (layer,) + (0,) * (nd - 1),
                        pipeline_mode=pl.Buffered(1))


def kernel(x, norm1_g, w_in, conv_a_w, conv_a_b, ln_a_g, ln_a_b, ln_b_g, ln_b_b, w_spatial, b_spatial,
           conv_c_w, conv_c_b, dt_bias, a_log, d_skip, norm_c_g, w_out, norm2_g, w_ff1, w_ff2, final_g):
    bsz, seq, d = x.shape
    depth = w_in.shape[0]
    aw, bw, cw = conv_a_w.shape[-1], ln_b_g.shape[-1], norm_c_g.shape[-1]
    n_heads_c = dt_bias.shape[-1]
    d_conv_c = conv_c_w.shape[-1]
    n_state = (d_conv_c - cw) // (2 * SSM_GROUPS)
    n_taps_a, n_taps_c = conv_a_w.shape[1], conv_c_w.shape[1]
    d_ff = w_ff1.shape[-1]
    T = min(MIX_ROWS, seq)
    assert seq % T == 0 and T % CHUNK == 0 and w_spatial.shape[-1] == CHUNK
    assert n_state == LANES and n_heads_c * HEAD_DIM == cw and n_heads_c <= LANES
    assert aw % MXU_DIM == 0 and bw % MXU_DIM == 0 and cw % (SSM_GROUPS * LANES) == 0
    assert n_taps_a - 1 <= A_HIST <= T and n_taps_c - 1 <= C_HIST
    assert w_in.shape[-1] == 2 * aw + 2 * bw + cw + d_conv_c + n_heads_c

    pw = w_in.shape[-1] - n_heads_c + LANES
    win = jnp.pad(w_in, ((0, 0), (0, 0), (0, pw - w_in.shape[-1]))).astype(BF16)
    wout = w_out.astype(BF16)
    w1, w2 = w_ff1.astype(BF16), w_ff2.astype(BF16)
    row3 = lambda a: a.reshape(depth, 1, a.shape[-1])
    lane_pad = lambda a: jnp.pad(a, ((0, 0), (0, LANES - a.shape[-1])))
    caw = jnp.pad(conv_a_w, ((0, 0), (0, A_HIST - n_taps_a), (0, 0)))
    bmat = jnp.repeat(jnp.swapaxes(b_spatial, 1, 2), HEAD_DIM, axis=2)
    dsk = jnp.repeat(d_skip, HEAD_DIM, axis=1)
    gidx = jnp.arange(MXU_DIM) // HEAD_DIM
    avg = (jnp.where(gidx[:, None] == gidx[None, :], 1.0 / HEAD_DIM, 0.0)).astype(BF16)

    cparams = pltpu.CompilerParams(dimension_semantics=("arbitrary", "arbitrary"),
                                   vmem_limit_bytes=VMEM_LIMIT_BYTES)
    mix_dims = (T, aw, bw, cw, n_state, n_taps_a, n_taps_c)
    x_spec = pl.BlockSpec((None, T, d), lambda b, s: (b, s, 0))
    avg_spec = pl.BlockSpec(avg.shape, lambda b, s: (0, 0), pipeline_mode=pl.Buffered(1))

    tok = bsz * seq
    tf = min(FFN_ROWS, tok)
    assert tok % tf == 0
    ff_chunk = min(1024, d_ff)
    assert d_ff % ff_chunk == 0
    gf = final_g.reshape(1, 1, d)

    for i in range(depth):
        mix_args = (norm1_g.reshape(depth, 1, d), win, caw, row3(conv_a_b), row3(ln_a_g), row3(ln_a_b),
                    row3(ln_b_g), row3(ln_b_b), w_spatial, bmat, conv_c_w, row3(conv_c_b),
                    row3(lane_pad(dt_bias)), row3(lane_pad(a_log)), row3(dsk), row3(norm_c_g), wout)
        x = pl.pallas_call(
            functools.partial(_mix_kernel, dims=mix_dims),
            grid=(bsz, seq // T),
            in_specs=[x_spec] + [_resident(a.shape, i) for a in mix_args] + [avg_spec],
            out_specs=x_spec,
            out_shape=jax.ShapeDtypeStruct((bsz, seq, d), F32),
            scratch_shapes=[
                pltpu.VMEM((T, pw), F32),
                pltpu.VMEM((T + A_HIST, aw), F32),
                pltpu.VMEM((T + C_HIST, d_conv_c), F32),
                pltpu.VMEM((n_state, cw), F32),
                pltpu.VMEM((T, aw + bw + cw), BF16),
            ],
            compiler_params=cparams,
            name=f"mix{i}",
        )(x, *mix_args, avg)

        last = i == depth - 1
        xf = pl.pallas_call(
            functools.partial(_ffn_kernel, ff_chunk=ff_chunk, final_norm=last),
            grid=(tok // tf,),
            in_specs=[pl.BlockSpec((tf, d), lambda t: (t, 0)),
                      _resident((depth, 1, d), i), _resident(w1.shape, i), _resident(w2.shape, i),
                      _resident(gf.shape, 0)],
            out_specs=pl.BlockSpec((tf, d), lambda t: (t, 0)),
            out_shape=jax.ShapeDtypeStruct((tok, d), F32),
            scratch_shapes=[pltpu.VMEM((tf, d_ff), BF16)],
            compiler_params=pltpu.CompilerParams(dimension_semantics=("arbitrary",),
                                                 vmem_limit_bytes=VMEM_LIMIT_BYTES),
            name=f"ffn{i}",
        )(x.reshape(tok, d), norm2_g.reshape(depth, 1, d), w1, w2, gf)
        x = xf.reshape(bsz, seq, d)
    return x
```

```python
import functools

import jax
import jax.numpy as jnp
from jax import lax
from jax.experimental import pallas as pl
from jax.experimental.pallas import tpu as pltpu

HEAD_DIM = 64
SSM_GROUPS = 2
CHUNK = 128
EPS = 1e-5
LANES = 128
SUBLANES = 8
BF16_ROWS = 16
MXU_DIM = 256
VMEM_LIMIT_BYTES = 56 * 1024 * 1024

MIX_ROWS = 256
FFN_ROWS = 512
A_HIST = 32
C_HIST = 8
CONV_ROWS = 64

F32 = jnp.float32
BF16 = jnp.bfloat16


def _dot(a, b):
    return jnp.dot(a, b, preferred_element_type=F32)


def _silu(x):
    return x * jax.nn.sigmoid(x)


def _gelu(x):
    return 0.5 * x * (1.0 + lax.erf(x * 0.7071067811865476))


def _rmsnorm(x, g):
    return x * lax.rsqrt(jnp.mean(x * x, axis=-1, keepdims=True) + EPS) * g


def _head_layernorm(x, g, b, avg):
    outs = []
    for c0 in range(0, x.shape[1], MXU_DIM):
        xs = x[:, c0:c0 + MXU_DIM]
        mu = _dot(xs.astype(BF16), avg)
        xc = xs - mu
        var = _dot((xc * xc).astype(BF16), avg)
        outs.append(xc * lax.rsqrt(var + EPS))
    return jnp.concatenate(outs, axis=1) * g + b


def _split3(x):
    p1 = x.astype(BF16)
    r1 = x - p1.astype(F32)
    p2 = r1.astype(BF16)
    r2 = r1 - p2.astype(F32)
    return p1, p2, r2.astype(BF16)


def _shift_rows(a, j):
    return pltpu.roll(a, a.shape[0] - j, axis=0)


def _causal_conv_a(abuf, sh, wt_ref, out, T, n_taps):
    a_all = abuf[...]
    base = A_HIST - (n_taps - 1)
    span = T + BF16_ROWS
    sh[0] = a_all.astype(BF16)
    sh[SUBLANES, 0:span] = a_all[SUBLANES:SUBLANES + span].astype(BF16)
    for j in range(1, SUBLANES):
        r = _shift_rows(a_all, j)
        sh[j, 0:span] = r[0:span].astype(BF16)
        sh[j + SUBLANES, 0:span] = r[SUBLANES:SUBLANES + span].astype(BF16)
    taps = [((base + k) % BF16_ROWS, (base + k) // BF16_ROWS) for k in range(n_taps)]
    for q in range(abuf.shape[1] // LANES):
        cols = slice(q * LANES, (q + 1) * LANES)
        ws = [wt_ref[k, :, cols] for k in range(n_taps)]

        for rs in range(0, T, BF16_ROWS):
            acc = jnp.zeros((BF16_ROWS, LANES), F32)
            for k, (j, m) in enumerate(taps):
                acc = acc + sh[j, pl.ds(rs + BF16_ROWS * m, BF16_ROWS), cols].astype(F32) * ws[k].astype(F32)
            out[pl.ds(rs, BF16_ROWS), cols] = acc


def _mix_kernel(x_ref, g1_ref, win_ref, cawt_ref, cab_ref, lag_ref, lab_ref, lbg_ref, lbb_ref,
                ws_ref, bmat_ref, ccw_ref, ccb_ref, dtb_ref, alog_ref, dsk_ref, ncg_ref, wout_ref,
                avg_ref, o_ref, proj, abuf, sh, cbuf, state, ybuf, *, dims):
    T, aw, bw, cw, n_state, n_taps_a, n_taps_c = dims
    n_chunks = T // CHUNK
    gw = cw // SSM_GROUPS
    o_bu, o_bv, o_z = 2 * aw, 2 * aw + bw, 2 * aw + 2 * bw
    o_xbc = o_z + cw
    o_dt = o_xbc + cw + 2 * SSM_GROUPS * n_state

    @pl.when(pl.program_id(1) == 0)
    def _():
        abuf[0:A_HIST, :] = jnp.zeros((A_HIST, aw), F32)
        cbuf[0:C_HIST, :] = jnp.zeros((C_HIST, cbuf.shape[1]), F32)
        state[...] = jnp.zeros(state.shape, F32)

    x = x_ref[...]
    hb = _rmsnorm(x, g1_ref[...]).astype(BF16)
    proj[...] = _dot(hb, win_ref[...])
    avg = avg_ref[...]

    abuf[A_HIST:A_HIST + T, :] = proj[:, 0:aw] * jax.nn.sigmoid(proj[:, aw:2 * aw])
    _causal_conv_a(abuf, sh, cawt_ref, proj, T, n_taps_a)
    abuf[0:A_HIST, :] = abuf[T:T + A_HIST, :]
    ya = _silu(_head_layernorm(proj[:, 0:aw] + cab_ref[...], lag_ref[...], lab_ref[...], avg))
    ybuf[:, 0:aw] = ya.astype(BF16)

    lane = lax.broadcasted_iota(jnp.int32, (CHUNK, LANES), 1)
    row = lax.broadcasted_iota(jnp.int32, (CHUNK, LANES), 0)
    lo = lane < HEAD_DIM
    causal = row >= lane
    u = _gelu(proj[:, o_bu:o_bu + bw])
    vn = _head_layernorm(_gelu(proj[:, o_bv:o_bv + bw]), lbg_ref[...], lbb_ref[...], avg)
    for j in range(bw // LANES):
        w_pair = jnp.concatenate(
            [jnp.where(causal, ws_ref[2 * j], 0.0), jnp.where(causal, ws_ref[2 * j + 1], 0.0)],
            axis=1).astype(BF16)
        cols = slice(j * LANES, (j + 1) * LANES)
        for c in range(n_chunks):
            rows = slice(c * CHUNK, (c + 1) * CHUNK)
            vp = vn[rows, cols]
            rhs = jnp.concatenate([jnp.where(lo, vp, 0.0).astype(BF16),
                                   jnp.where(lo, 0.0, vp).astype(BF16)], axis=0)
            mixv = _dot(w_pair, rhs) + bmat_ref[:, cols]
            ybuf[rows, aw + j * LANES:aw + (j + 1) * LANES] = (u[rows, cols] * mixv).astype(BF16)

    cbuf[C_HIST:C_HIST + T, :] = proj[:, o_xbc:o_dt]
    c_all = cbuf[...]
    xbc = ccb_ref[...] + ccw_ref[n_taps_c - 1:n_taps_c, :] * c_all[C_HIST:C_HIST + T]
    for k in range(n_taps_c - 1):
        xbc = xbc + ccw_ref[k:k + 1, :] * _shift_rows(c_all, C_HIST - (n_taps_c - 1) + k)[0:T]
    cbuf[0:C_HIST, :] = c_all[T:T + C_HIST]
    xbc = _silu(xbc)
    dt = jax.nn.softplus(proj[:, o_dt:o_dt + LANES] + dtb_ref[...])
    dta = dt * (-jnp.exp(alog_ref[...]))
    tri = jnp.where(causal, 1.0, 0.0).astype(BF16)
    n_pairs = cw // LANES
    pairs_per_group = n_pairs // SSM_GROUPS
    heads_valid = 2 * n_pairs
    for c in range(n_chunks):
        rows = slice(c * CHUNK, (c + 1) * CHUNK)
        p1, p2, p3 = _split3(dta[rows, :])
        a_cs = _dot(tri, p1) + _dot(tri, p2) + _dot(tri, p3)
        a_cs_t = a_cs.T[0:heads_valid, :]
        dt_t = dt[rows, :].T[0:heads_valid, :]
        last = jnp.broadcast_to(a_cs_t[:, CHUNK - 1:CHUNK], a_cs_t.shape)
        w_t = jnp.exp(last - a_cs_t) * dt_t
        for g in range(SSM_GROUPS):
            bm = xbc[rows, cw + g * n_state:cw + (g + 1) * n_state]
            cm = xbc[rows, cw + (SSM_GROUPS + g) * n_state:cw + (SSM_GROUPS + g + 1) * n_state]
            cb = lax.dot_general(cm.astype(BF16), bm.astype(BF16), (((1,), (1,)), ((), ())),
                                 preferred_element_type=F32)
            bm_t = bm.T
            ys = []
            ss = jnp.zeros((CHUNK, 1), F32)
            for jj in range(pairs_per_group):
                j = g * pairs_per_group + jj
                cols = slice(j * LANES, (j + 1) * LANES)
                m1, m2, m3t, cd = [], [], [], []
                for h in (2 * j, 2 * j + 1):
                    col_a = jnp.broadcast_to(a_cs[:, h:h + 1], (CHUNK, LANES))
                    seg = col_a - a_cs_t[h:h + 1, :]
                    decay = jnp.exp(jnp.where(causal, seg, -jnp.inf))
                    m1.append((cb * decay * dt_t[h:h + 1, :]).astype(BF16))
                    m2.append((cm * jnp.exp(col_a)).astype(BF16))
                    m3t.append((bm_t * w_t[h:h + 1, :]).astype(BF16))
                    cd.append(jnp.exp(col_a[CHUNK - 1:CHUNK, :]))
                xs = xbc[rows, cols]
                s_prev = state[:, cols]
                x_lo = jnp.where(lo, xs, 0.0).astype(BF16)
                x_hi = jnp.where(lo, 0.0, xs).astype(BF16)
                s_lo = jnp.where(lo, s_prev, 0.0).astype(BF16)
                s_hi = jnp.where(lo, 0.0, s_prev).astype(BF16)
                y = _dot(jnp.concatenate(m1 + m2, axis=1),
                         jnp.concatenate([x_lo, x_hi, s_lo, s_hi], axis=0))
                y = y + dsk_ref[:, cols] * xs
                y = y * _silu(proj[rows, o_z + j * LANES:o_z + (j + 1) * LANES])
                s_add = _dot(jnp.concatenate(m3t, axis=1), jnp.concatenate([x_lo, x_hi], axis=0))
                state[:, cols] = s_prev * jnp.where(lo[0:1, :], cd[0], cd[1]) + s_add
                ys.append(y)
                ss = ss + jnp.sum(y * y, axis=-1, keepdims=True)
            rstd = lax.rsqrt(ss * (1.0 / gw) + EPS)
            for jj in range(pairs_per_group):
                j = g * pairs_per_group + jj
                cols = slice(j * LANES, (j + 1) * LANES)
                ybuf[rows, aw + bw + j * LANES:aw + bw + (j + 1) * LANES] = (
                    ys[jj] * rstd * ncg_ref[:, cols]).astype(BF16)

    o_ref[...] = x + _dot(ybuf[...], wout_ref[...])


def _ffn_kernel(x_ref, g2_ref, w1_ref, w2_ref, gf_ref, o_ref, hbuf, *, ff_chunk, final_norm):
    x = x_ref[...]
    hb = _rmsnorm(x, g2_ref[...]).astype(BF16)
    for f0 in range(0, w1_ref.shape[1], ff_chunk):
        a = jnp.maximum(_dot(hb, w1_ref[:, f0:f0 + ff_chunk]), 0.0)
        hbuf[:, f0:f0 + ff_chunk] = (a * a).astype(BF16)
    y = x + _dot(hbuf[...], w2_ref[...])
    if final_norm:
        y = _rmsnorm(y, gf_ref[...])
    o_ref[...] = y


def _resident(shape, layer):
    nd = len(shape)
    return pl.BlockSpec((None,) + tuple(shape[1:]), lambda *_: (layer,) + (0,) * (nd - 1),
                        pipeline_mode=pl.Buffered(1))


def kernel(x, norm1_g, w_in, conv_a_w, conv_a_b, ln_a_g, ln_a_b, ln_b_g, ln_b_b, w_spatial, b_spatial,
           conv_c_w, conv_c_b, dt_bias, a_log, d_skip, norm_c_g, w_out, norm2_g, w_ff1, w_ff2, final_g):
    bsz, seq, d = x.shape
    depth = w_in.shape[0]
    aw, bw, cw = conv_a_w.shape[-1], ln_b_g.shape[-1], norm_c_g.shape[-1]
    n_heads_c = dt_bias.shape[-1]
    d_conv_c = conv_c_w.shape[-1]
    n_state = (d_conv_c - cw) // (2 * SSM_GROUPS)
    n_taps_a, n_taps_c = conv_a_w.shape[1], conv_c_w.shape[1]
    d_ff = w_ff1.shape[-1]
    T = min(MIX_ROWS, seq)
    assert seq % T == 0 and T % CHUNK == 0 and T % CONV_ROWS == 0 and w_spatial.shape[-1] == CHUNK
    assert n_state == LANES and n_heads_c * HEAD_DIM == cw and n_heads_c <= LANES
    assert aw % MXU_DIM == 0 and bw % MXU_DIM == 0 and cw % (SSM_GROUPS * LANES) == 0
    assert n_taps_a - 1 <= A_HIST <= T and A_HIST % BF16_ROWS == 0 and n_taps_c - 1 <= C_HIST
    assert w_in.shape[-1] == 2 * aw + 2 * bw + cw + d_conv_c + n_heads_c

    pw = w_in.shape[-1] - n_heads_c + LANES
    win = jnp.pad(w_in, ((0, 0), (0, 0), (0, pw - w_in.shape[-1]))).astype(BF16)
    wout = w_out.astype(BF16)
    w1, w2 = w_ff1.astype(BF16), w_ff2.astype(BF16)
    row3 = lambda a: a.reshape(depth, 1, a.shape[-1])
    lane_pad = lambda a: jnp.pad(a, ((0, 0), (0, LANES - a.shape[-1])))
    cawt = jnp.broadcast_to(conv_a_w[:, :, None, :], (depth, n_taps_a, BF16_ROWS, aw)).astype(BF16)
    bmat = jnp.repeat(jnp.swapaxes(b_spatial, 1, 2), HEAD_DIM, axis=2)
    dsk = jnp.repeat(d_skip, HEAD_DIM, axis=1)
    gidx = jnp.arange(MXU_DIM) // HEAD_DIM
    avg = (jnp.where(gidx[:, None] == gidx[None, :], 1.0 / HEAD_DIM, 0.0)).astype(BF16)

    cparams = pltpu.CompilerParams(dimension_semantics=("arbitrary", "arbitrary"),
                                   vmem_limit_bytes=VMEM_LIMIT_BYTES)
    mix_dims = (T, aw, bw, cw, n_state, n_taps_a, n_taps_c)
    x_spec = pl.BlockSpec((None, T, d), lambda b, s: (b, s, 0))
    avg_spec = pl.BlockSpec(avg.shape, lambda b, s: (0, 0), pipeline_mode=pl.Buffered(1))

    tok = bsz * seq
    tf = min(FFN_ROWS, tok)
    assert tok % tf == 0
    ff_chunk = min(1024, d_ff)
    assert d_ff % ff_chunk == 0
    gf = final_g.reshape(1, 1, d)

    for i in range(depth):
        mix_args = (norm1_g.reshape(depth, 1, d), win, cawt, row3(conv_a_b), row3(ln_a_g), row3(ln_a_b),
                    row3(ln_b_g), row3(ln_b_b), w_spatial, bmat, conv_c_w, row3(conv_c_b),
                    row3(lane_pad(dt_bias)), row3(lane_pad(a_log)), row3(dsk), row3(norm_c_g), wout)
        x = pl.pallas_call(
            functools.partial(_mix_kernel, dims=mix_dims),
            grid=(bsz, seq // T),
            in_specs=[x_spec] + [_resident(a.shape, i) for a in mix_args] + [avg_spec],
            out_specs=x_spec,
            out_shape=jax.ShapeDtypeStruct((bsz, seq, d), F32),
            scratch_shapes=[
                pltpu.VMEM((T, pw), F32),
                pltpu.VMEM((T + A_HIST, aw), F32),
                pltpu.VMEM((BF16_ROWS, T + A_HIST, aw), BF16),
                pltpu.VMEM((T + C_HIST, d_conv_c), F32),
                pltpu.VMEM((n_state, cw), F32),
                pltpu.VMEM((T, aw + bw + cw), BF16),
            ],
            compiler_params=cparams,
            name=f"mix{i}",
        )(x, *mix_args, avg)

        last = i == depth - 1
        xf = pl.pallas_call(
            functools.partial(_ffn_kernel, ff_chunk=ff_chunk, final_norm=last),
            grid=(tok // tf,),
            in_specs=[pl.BlockSpec((tf, d), lambda t: (t, 0)),
                      _resident((depth, 1, d), i), _resident(w1.shape, i), _resident(w2.shape, i),
                      _resident(gf.shape, 0)],
            out_specs=pl.BlockSpec((tf, d), lambda t: (t, 0)),
            out_shape=jax.ShapeDtypeStruct((tok, d), F32),
            scratch_shapes=[pltpu.VMEM((tf, d_ff), BF16)],
            compiler_params=pltpu.CompilerParams(dimension_semantics=("arbitrary",),
                                                 vmem_limit_bytes=VMEM_LIMIT_BYTES),
            name=f"ffn{i}",
        )(x.reshape(tok, d), norm2_g.reshape(depth, 1, d), w1, w2, gf)
        x = xf.reshape(bsz, seq, d)
    return x
```

```python
import functools

import jax
import jax.numpy as jnp
from jax import lax
from jax.experimental import pallas as pl
from jax.experimental.pallas import tpu as pltpu

HEAD_DIM = 64
SSM_GROUPS = 2
CHUNK = 128
EPS = 1e-5
LANES = 128
SUBLANES = 8
BF16_ROWS = 16
MXU_DIM = 256
VMEM_LIMIT_BYTES = 56 * 1024 * 1024

MIX_ROWS = 512
FFN_ROWS = 512
A_HIST = 32
C_HIST = 8
PIECE = 512

F32 = jnp.float32
BF16 = jnp.bfloat16


def _dot(a, b):
    return jnp.dot(a, b, preferred_element_type=F32)


def _silu(x):
    return x * jax.nn.sigmoid(x)


def _gelu(x):
    return 0.5 * x * (1.0 + lax.erf(x * 0.7071067811865476))


def _rmsnorm(x, g):
    return x * lax.rsqrt(jnp.mean(x * x, axis=-1, keepdims=True) + EPS) * g


def _head_layernorm(x, g, b, avg):
    outs = []
    for c0 in range(0, x.shape[1], MXU_DIM):
        xs = x[:, c0:c0 + MXU_DIM]
        mu = _dot(xs.astype(BF16), avg)
        xc = xs - mu
        var = _dot((xc * xc).astype(BF16), avg)
        outs.append(xc * lax.rsqrt(var + EPS))
    return jnp.concatenate(outs, axis=1) * g + b


def _split3(x):
    p1 = x.astype(BF16)
    r1 = x - p1.astype(F32)
    p2 = r1.astype(BF16)
    r2 = r1 - p2.astype(F32)
    return p1, p2, r2.astype(BF16)


def _shift_rows(a, j):
    return pltpu.roll(a, a.shape[0] - j, axis=0)


def _causal_conv_a(a_all, sh, wt_ref, out, T, n_taps):
    base = A_HIST - (n_taps - 1)
    span = T + BF16_ROWS
    sh[0] = a_all.astype(BF16)
    sh[SUBLANES, 0:span] = a_all[SUBLANES:SUBLANES + span].astype(BF16)
    for j in range(1, SUBLANES):
        r = _shift_rows(a_all, j)
        sh[j, 0:span] = r[0:span].astype(BF16)
        sh[j + SUBLANES, 0:span] = r[SUBLANES:SUBLANES + span].astype(BF16)
    taps = [((base + k) % BF16_ROWS, (base + k) // BF16_ROWS) for k in range(n_taps)]
    for q in range(a_all.shape[1] // LANES):
        cols = slice(q * LANES, (q + 1) * LANES)
        ws = [wt_ref[k, :, cols] for k in range(n_taps)]

        for rs in range(0, T, BF16_ROWS):
            acc = jnp.zeros((BF16_ROWS, LANES), F32)
            for k, (j, m) in enumerate(taps):
                acc = acc + sh[j, pl.ds(rs + BF16_ROWS * m, BF16_ROWS), cols].astype(F32) * ws[k].astype(F32)
            out[pl.ds(rs, BF16_ROWS), cols] = acc


def _mix_kernel(x_ref, g1_ref, win_ref, cawt_ref, cab_ref, lag_ref, lab_ref, lbg_ref, lbb_ref,
                ws_ref, bmat_ref, ccw_ref, ccb_ref, dtb_ref, alog_ref, dsk_ref, ncg_ref, wout_ref,
                avg_ref, o_ref, ahist, sh, convo, ubuf, vbuf, chist, xcv, zbuf, state, ybuf, *, dims):
    T, aw, bw, cw, n_state, n_taps_a, n_taps_c = dims
    n_chunks = T // CHUNK
    gw = cw // SSM_GROUPS
    d_conv_c = chist.shape[1]
    o_bu, o_bv, o_z = 2 * aw, 2 * aw + bw, 2 * aw + 2 * bw
    o_xbc = o_z + cw
    o_dt = o_xbc + d_conv_c

    @pl.when(pl.program_id(1) == 0)
    def _():
        ahist[...] = jnp.zeros(ahist.shape, F32)
        chist[...] = jnp.zeros(chist.shape, F32)
        state[...] = jnp.zeros(state.shape, F32)

    hb = _rmsnorm(x_ref[...], g1_ref[...]).astype(BF16)

    def proj(c0, c1):
        return _dot(hb, win_ref[:, c0:c1])

    for c0 in range(0, d_conv_c, PIECE):
        cs = slice(c0, c0 + PIECE)
        raw = proj(o_xbc + c0, o_xbc + c0 + PIECE)
        c_all = jnp.concatenate([chist[:, cs], raw], axis=0)
        part = ccb_ref[:, cs] + ccw_ref[n_taps_c - 1:n_taps_c, cs] * raw
        for k in range(n_taps_c - 1):
            part = part + ccw_ref[k:k + 1, cs] * _shift_rows(c_all, C_HIST - (n_taps_c - 1) + k)[0:T]
        xcv[:, cs] = _silu(part)
        chist[:, cs] = raw[T - C_HIST:T]

    glu = proj(0, aw) * jax.nn.sigmoid(proj(aw, 2 * aw))
    a_all = jnp.concatenate([ahist[...], glu], axis=0)
    ahist[...] = glu[T - A_HIST:T]
    zbuf[...] = _silu(proj(o_z, o_z + cw))
    ubuf[...] = _gelu(proj(o_bu, o_bu + bw))
    vbuf[...] = _gelu(proj(o_bv, o_bv + bw))
    dt = jax.nn.softplus(proj(o_dt, o_dt + LANES) + dtb_ref[...])
    avg = avg_ref[...]

    _causal_conv_a(a_all, sh, cawt_ref, convo, T, n_taps_a)
    ya = _silu(_head_layernorm(convo[...] + cab_ref[...], lag_ref[...], lab_ref[...], avg))
    out_ab = _dot(ya.astype(BF16), wout_ref[0:aw, :])

    lane = lax.broadcasted_iota(jnp.int32, (CHUNK, LANES), 1)
    row = lax.broadcasted_iota(jnp.int32, (CHUNK, LANES), 0)
    lo = lane < HEAD_DIM
    causal = row >= lane
    u = ubuf[...]
    vn = _head_layernorm(vbuf[...], lbg_ref[...], lbb_ref[...], avg)
    for j in range(bw // LANES):
        w_pair = jnp.concatenate(
            [jnp.where(causal, ws_ref[2 * j], 0.0), jnp.where(causal, ws_ref[2 * j + 1], 0.0)],
            axis=1).astype(BF16)
        cols = slice(j * LANES, (j + 1) * LANES)
        for c in range(n_chunks):
            rows = slice(c * CHUNK, (c + 1) * CHUNK)
            vp = vn[rows, cols]
            rhs = jnp.concatenate([jnp.where(lo, vp, 0.0).astype(BF16),
                                   jnp.where(lo, 0.0, vp).astype(BF16)], axis=0)
            mixv = _dot(w_pair, rhs) + bmat_ref[:, cols]
            ybuf[rows, cols] = (u[rows, cols] * mixv).astype(BF16)
    out_ab = out_ab + _dot(ybuf[:, 0:bw], wout_ref[aw:aw + bw, :])

    dta = dt * (-jnp.exp(alog_ref[...]))
    tri = jnp.where(causal, 1.0, 0.0).astype(BF16)
    n_pairs = cw // LANES
    pairs_per_group = n_pairs // SSM_GROUPS
    heads_valid = 2 * n_pairs
    for c in range(n_chunks):
        rows = slice(c * CHUNK, (c + 1) * CHUNK)
        p1, p2, p3 = _split3(dta[rows, :])
        a_cs = _dot(tri, p1) + _dot(tri, p2) + _dot(tri, p3)
        a_cs_t = a_cs.T[0:heads_valid, :]
        dt_t = dt[rows, :].T[0:heads_valid, :]
        last = jnp.broadcast_to(a_cs_t[:, CHUNK - 1:CHUNK], a_cs_t.shape)
        w_t = jnp.exp(last - a_cs_t) * dt_t
        for g in range(SSM_GROUPS):
            bm = xcv[rows, cw + g * n_state:cw + (g + 1) * n_state]
            cm = xcv[rows, cw + (SSM_GROUPS + g) * n_state:cw + (SSM_GROUPS + g + 1) * n_state]
            cb = lax.dot_general(cm.astype(BF16), bm.astype(BF16), (((1,), (1,)), ((), ())),
                                 preferred_element_type=F32)
            bm_t = bm.T
            ys = []
            ss = jnp.zeros((CHUNK, 1), F32)
            for jj in range(pairs_per_group):
                j = g * pairs_per_group + jj
                cols = slice(j * LANES, (j + 1) * LANES)
                m1, m2, m3t, cd = [], [], [], []
                for h in (2 * j, 2 * j + 1):
                    col_a = jnp.broadcast_to(a_cs[:, h:h + 1], (CHUNK, LANES))
                    seg = col_a - a_cs_t[h:h + 1, :]
                    decay = jnp.exp(jnp.where(causal, seg, -jnp.inf))
                    m1.append((cb * decay * dt_t[h:h + 1, :]).astype(BF16))
                    m2.append((cm * jnp.exp(col_a)).astype(BF16))
                    m3t.append((bm_t * w_t[h:h + 1, :]).astype(BF16))
                    cd.append(jnp.exp(col_a[CHUNK - 1:CHUNK, :]))
                xs = xcv[rows, cols]
                s_prev = state[:, cols]
                x_lo = jnp.where(lo, xs, 0.0).astype(BF16)
                x_hi = jnp.where(lo, 0.0, xs).astype(BF16)
                s_lo = jnp.where(lo, s_prev, 0.0).astype(BF16)
                s_hi = jnp.where(lo, 0.0, s_prev).astype(BF16)
                y = _dot(jnp.concatenate(m1 + m2, axis=1),
                         jnp.concatenate([x_lo, x_hi, s_lo, s_hi], axis=0))
                y = y + dsk_ref[:, cols] * xs
                y = y * zbuf[rows, cols]
                s_add = _dot(jnp.concatenate(m3t, axis=1), jnp.concatenate([x_lo, x_hi], axis=0))
                state[:, cols] = s_prev * jnp.where(lo[0:1, :], cd[0], cd[1]) + s_add
                ys.append(y)
                ss = ss + jnp.sum(y * y, axis=-1, keepdims=True)
            rstd = lax.rsqrt(ss * (1.0 / gw) + EPS)
            for jj in range(pairs_per_group):
                j = g * pairs_per_group + jj
                cols = slice(j * LANES, (j + 1) * LANES)
                ybuf[rows, bw + j * LANES:bw + (j + 1) * LANES] = (
                    ys[jj] * rstd * ncg_ref[:, cols]).astype(BF16)
        o_ref[rows, :] = (x_ref[rows, :] + out_ab[rows, :]
                          + _dot(ybuf[rows, bw:bw + cw], wout_ref[aw + bw:aw + bw + cw, :]))


def _ffn_kernel(x_ref, g2_ref, w1_ref, w2_ref, gf_ref, o_ref, hbuf, *, ff_chunk, final_norm):
    x = x_ref[...]
    hb = _rmsnorm(x, g2_ref[...]).astype(BF16)
    for f0 in range(0, w1_ref.shape[1], ff_chunk):
        a = jnp.maximum(_dot(hb, w1_ref[:, f0:f0 + ff_chunk]), 0.0)
        hbuf[:, f0:f0 + ff_chunk] = (a * a).astype(BF16)
    y = x + _dot(hbuf[...], w2_ref[...])
    if final_norm:
        y = _rmsnorm(y, gf_ref[...])
    o_ref[...] = y


def _resident(shape, layer):
    nd = len(shape)
    return pl.BlockSpec((None,) + tuple(shape[1:]), lambda *_: (layer,) + (0,) * (nd - 1),
                        pipeline_mode=pl.Buffered(1))


def kernel(x, norm1_g, w_in, conv_a_w, conv_a_b, ln_a_g, ln_a_b, ln_b_g, ln_b_b, w_spatial, b_spatial,
           conv_c_w, conv_c_b, dt_bias, a_log, d_skip, norm_c_g, w_out, norm2_g, w_ff1, w_ff2, final_g):
    bsz, seq, d = x.shape
    depth = w_in.shape[0]
    aw, bw, cw = conv_a_w.shape[-1], ln_b_g.shape[-1], norm_c_g.shape[-1]
    n_heads_c = dt_bias.shape[-1]
    d_conv_c = conv_c_w.shape[-1]
    n_state = (d_conv_c - cw) // (2 * SSM_GROUPS)
    n_taps_a, n_taps_c = conv_a_w.shape[1], conv_c_w.shape[1]
    d_ff = w_ff1.shape[-1]
    T = min(MIX_ROWS, seq)
    assert seq % T == 0 and T % CHUNK == 0 and w_spatial.shape[-1] == CHUNK
    assert n_state == LANES and n_heads_c * HEAD_DIM == cw and n_heads_c <= LANES
    assert aw % MXU_DIM == 0 and bw % MXU_DIM == 0 and cw % (SSM_GROUPS * LANES) == 0
    assert n_taps_a - 1 <= A_HIST <= T and A_HIST % BF16_ROWS == 0 and n_taps_c - 1 <= C_HIST <= T
    assert d_conv_c % PIECE == 0
    assert w_in.shape[-1] == 2 * aw + 2 * bw + cw + d_conv_c + n_heads_c

    pw = w_in.shape[-1] - n_heads_c + LANES
    win = jnp.pad(w_in, ((0, 0), (0, 0), (0, pw - w_in.shape[-1]))).astype(BF16)
    wout = w_out.astype(BF16)
    w1, w2 = w_ff1.astype(BF16), w_ff2.astype(BF16)
    row3 = lambda a: a.reshape(depth, 1, a.shape[-1])
    lane_pad = lambda a: jnp.pad(a, ((0, 0), (0, LANES - a.shape[-1])))
    cawt = jnp.broadcast_to(conv_a_w[:, :, None, :], (depth, n_taps_a, BF16_ROWS, aw)).astype(BF16)
    bmat = jnp.repeat(jnp.swapaxes(b_spatial, 1, 2), HEAD_DIM, axis=2)
    dsk = jnp.repeat(d_skip, HEAD_DIM, axis=1)
    gidx = jnp.arange(MXU_DIM) // HEAD_DIM
    avg = (jnp.where(gidx[:, None] == gidx[None, :], 1.0 / HEAD_DIM, 0.0)).astype(BF16)

    cparams = pltpu.CompilerParams(dimension_semantics=("arbitrary", "arbitrary"),
                                   vmem_limit_bytes=VMEM_LIMIT_BYTES)
    mix_dims = (T, aw, bw, cw, n_state, n_taps_a, n_taps_c)
    x_spec = pl.BlockSpec((None, T, d), lambda b, s: (b, s, 0))
    avg_spec = pl.BlockSpec(avg.shape, lambda b, s: (0, 0), pipeline_mode=pl.Buffered(1))

    tok = bsz * seq
    tf = min(FFN_ROWS, tok)
    assert tok % tf == 0
    ff_chunk = min(1024, d_ff)
    assert d_ff % ff_chunk == 0
    gf = final_g.reshape(1, 1, d)

    for i in range(depth):
        mix_args = (norm1_g.reshape(depth, 1, d), win, cawt, row3(conv_a_b), row3(ln_a_g), row3(ln_a_b),
                    row3(ln_b_g), row3(ln_b_b), w_spatial, bmat, conv_c_w, row3(conv_c_b),
                    row3(lane_pad(dt_bias)), row3(lane_pad(a_log)), row3(dsk), row3(norm_c_g), wout)
        x = pl.pallas_call(
            functools.partial(_mix_kernel, dims=mix_dims),
            grid=(bsz, seq // T),
            in_specs=[x_spec] + [_resident(a.shape, i) for a in mix_args] + [avg_spec],
            out_specs=x_spec,
            out_shape=jax.ShapeDtypeStruct((bsz, seq, d), F32),
            scratch_shapes=[
                pltpu.VMEM((A_HIST, aw), F32),
                pltpu.VMEM((BF16_ROWS, T + A_HIST, aw), BF16),
                pltpu.VMEM((T, aw), F32),
                pltpu.VMEM((T, bw), F32),
                pltpu.VMEM((T, bw), F32),
                pltpu.VMEM((C_HIST, d_conv_c), F32),
                pltpu.VMEM((T, d_conv_c), F32),
                pltpu.VMEM((T, cw), F32),
                pltpu.VMEM((n_state, cw), F32),
                pltpu.VMEM((T, bw + cw), BF16),
            ],
            compiler_params=cparams,
            name=f"mix{i}",
        )(x, *mix_args, avg)

        last = i == depth - 1
        xf = pl.pallas_call(
            functools.partial(_ffn_kernel, ff_chunk=ff_chunk, final_norm=last),
            grid=(tok // tf,),
            in_specs=[pl.BlockSpec((tf, d), lambda t: (t, 0)),
                      _resident((depth, 1, d), i), _resident(w1.shape, i), _resident(w2.shape, i),
                      _resident(gf.shape, 0)],
            out_specs=pl.BlockSpec((tf, d), lambda t: (t, 0)),
            out_shape=jax.ShapeDtypeStruct((tok, d), F32),
            scratch_shapes=[pltpu.VMEM((tf, d_ff), BF16)],
            compiler_params=pltpu.CompilerParams(dimension_semantics=("arbitrary",),
                                                 vmem_limit_bytes=VMEM_LIMIT_BYTES),
            name=f"ffn{i}",
        )(x.reshape(tok, d), norm2_g.reshape(depth, 1, d), w1, w2, gf)
        x = xf.reshape(bsz, seq, d)
    return x
```
